```python
import functools
import jax, jax.numpy as jnp
from jax import lax
import numpy as np

D_MODEL = 1024
BATCH = 8
SEQ = 8192
DEPTH = 1
DEC_BATCH = 128
DEC_SEQ = 4
PAST_LEN = 8192
PAGE_SIZE = 128

MIX_WIDTH = D_MODEL
GLA_WIDTH = MIX_WIDTH // 2
GLA_HEADS = 4
GLA_DV = GLA_WIDTH // GLA_HEADS
GLA_DK = GLA_DV // 2
GLA_KEY = GLA_HEADS * GLA_DK
GLA_GATE_RANK = 16
GLA_GATE_TAU = 16.0
GLA_CHUNK = 64
DSA_WIDTH = MIX_WIDTH - GLA_WIDTH
DSA_HEADS = 8
DSA_HEAD_DIM = DSA_WIDTH // DSA_HEADS
DSA_KV_HEADS = 2
DSA_KV_WIDTH = DSA_KV_HEADS * DSA_HEAD_DIM
IDX_HEADS = 4
IDX_DIM = 64
TOPK_MAX = 256
Q_BLOCK = 128
N_GROUPS = 4
EXPERTS_PER_GROUP = 4
N_EXPERTS = N_GROUPS * EXPERTS_PER_GROUP
TOP_K_IN_GROUP = 2
D_EXPERT = 256
EPS = 1e-6
IN_SIZES = (GLA_KEY, GLA_KEY, GLA_WIDTH, GLA_GATE_RANK, GLA_WIDTH,
            DSA_WIDTH, DSA_KV_WIDTH, DSA_KV_WIDTH, IDX_HEADS * IDX_DIM, IDX_DIM, IDX_HEADS)
IN_TOTAL = sum(IN_SIZES)

kernel_name = "hymba_gla_dsa_hmoe_adaln_step"


def rms_norm(x, g):
    xf = x.astype(jnp.float32)
    y = xf * lax.rsqrt(jnp.mean(xf * xf, axis=-1, keepdims=True) + EPS)
    return (y * g.astype(jnp.float32)).astype(x.dtype)


def ada_modulation(c, w_ada, b_ada):
    m = jax.nn.silu(c) @ w_ada + b_ada
    return [t[:, None, :] for t in jnp.split(m, 6, axis=-1)]


def modulate(h, shift, scale):
    return h * (1.0 + scale) + shift


def split_columns(z):
    parts, start = [], 0
    for size in IN_SIZES:
        parts.append(z[..., start:start + size])
        start += size
    return parts


def project_heads(h, w_in, w_gla_gate2, b_gla_gate):
    B, T = h.shape[:2]
    gq, gk, gv, glr, gr, dq, dk, dv, iq, ik, iw = split_columns(h @ w_in)
    gq = gq.reshape(B, T, GLA_HEADS, GLA_DK) * GLA_DK ** -0.5
    gk = gk.reshape(B, T, GLA_HEADS, GLA_DK)
    gv = gv.reshape(B, T, GLA_HEADS, GLA_DV)
    glog = (jax.nn.log_sigmoid((glr @ w_gla_gate2 + b_gla_gate).astype(jnp.float32))
            / GLA_GATE_TAU).reshape(B, T, GLA_HEADS, GLA_DK)
    dq = dq.reshape(B, T, DSA_HEADS, DSA_HEAD_DIM)
    dk = dk.reshape(B, T, DSA_KV_HEADS, DSA_HEAD_DIM)
    dv = dv.reshape(B, T, DSA_KV_HEADS, DSA_HEAD_DIM)
    iq = iq.reshape(B, T, IDX_HEADS, IDX_DIM)
    return gq, gk, gv, glog, gr, dq, dk, dv, iq, ik, iw


def gla_chunk(state, q, k, v, logg):
    C = q.shape[2]
    b = jnp.cumsum(logg, axis=2)
    causal = jnp.tril(jnp.ones((C, C), dtype=bool))
    inter = jnp.einsum('bhtk,bhkv->bhtv', q * jnp.exp(b), state)
    diff = b[:, :, :, None, :] - b[:, :, None, :, :]
    decay = jnp.exp(jnp.where(causal[:, :, None], diff, -jnp.inf))
    attn = jnp.einsum('bhtk,bhtsk,bhsk->bhts', q, decay, k)
    o = inter + jnp.einsum('bhts,bhsv->bhtv', attn, v)
    b_last = b[:, :, -1, :]
    new_state = (jnp.exp(b_last)[..., None] * state
                 + jnp.einsum('bhsk,bhsv->bhkv', k * jnp.exp(b_last[:, :, None, :] - b), v))
    return new_state, o


def gla_scan(state0, q, k, v, logg, chunk):
    B, T, H, _ = q.shape
    n = T // chunk

    def to_chunks(a):
        return a.astype(jnp.float32).reshape(B, n, chunk, H, a.shape[-1]).transpose(1, 0, 3, 2, 4)

    final, o = lax.scan(lambda s, xs: gla_chunk(s, *xs), state0.astype(jnp.float32),
                        (to_chunks(q), to_chunks(k), to_chunks(v), to_chunks(logg)))
    o = o.transpose(1, 0, 3, 2, 4).reshape(B, T, H, GLA_DV)
    return final, o


def gla_output(o, gr, gla_norm_g, dtype):
    B, T = o.shape[:2]
    on = o * lax.rsqrt(jnp.mean(o * o, axis=-1, keepdims=True) + EPS) * gla_norm_g.astype(jnp.float32)
    out = on * jax.nn.silu(gr.astype(jnp.float32)).reshape(B, T, GLA_HEADS, GLA_DV)
    return out.reshape(B, T, GLA_WIDTH).astype(dtype)


def indexer_scores(iq, iw, ik):
    logits = jax.nn.relu(jnp.einsum('bthd,bsd->bths', iq, ik).astype(jnp.float32) * IDX_DIM ** -0.5)
    return jnp.einsum('bths,bth->bts', logits, iw.astype(jnp.float32) * IDX_HEADS ** -0.5)


def sparse_attend(q, k_sel, v_sel, valid):
    B, T = q.shape[:2]
    qg = q.reshape(B, T, DSA_KV_HEADS, DSA_HEADS // DSA_KV_HEADS, DSA_HEAD_DIM)
    s = jnp.einsum('btngd,btsnd->btngs', qg, k_sel).astype(jnp.float32) * DSA_HEAD_DIM ** -0.5
    s = jnp.where(valid[:, :, None, None, :], s, -jnp.inf)
    p = jax.nn.softmax(s, axis=-1)
    o = jnp.einsum('btngs,btsnd->btngd', p.astype(v_sel.dtype), v_sel)
    return o.reshape(B, T, DSA_WIDTH)


_gather_rows = jax.vmap(lambda a, i: a[i])


def dsa_prompt(dq, dk, dv, iq, ik, iw):
    B, S = dq.shape[:2]
    topk = min(TOPK_MAX, S // 4)
    nblk = S // Q_BLOCK
    kpos = jnp.arange(S)

    def blockify(a):
        return jnp.moveaxis(a.reshape((B, nblk, Q_BLOCK) + a.shape[2:]), 1, 0)

    def one_block(xs):
        q_b, iq_b, iw_b, pos_b = xs
        sc = indexer_scores(iq_b, iw_b, ik)
        sc = jnp.where(kpos[None, None, :] <= pos_b[None, :, None], sc, -jnp.inf)
        _, idx = lax.top_k(sc, topk)
        valid = idx <= pos_b[None, :, None]
        return sparse_attend(q_b, _gather_rows(dk, idx), _gather_rows(dv, idx), valid)

    out = lax.map(one_block, (blockify(dq), blockify(iq), blockify(iw),
                              jnp.arange(S).reshape(nblk, Q_BLOCK)))
    return jnp.moveaxis(out, 0, 1).reshape(B, S, DSA_WIDTH)


def dsa_sample(dq, dk, dv, iq, ik, iw, cache_k, cache_v, cache_ik, page_table):
    DB, T = dq.shape[:2]
    past = page_table.shape[1] * PAGE_SIZE
    L = past + T
    topk = min(TOPK_MAX, L // 4)
    ik_past = cache_ik[page_table].reshape(DB, past, IDX_DIM)
    ik_all = jnp.concatenate([ik_past, ik.astype(ik_past.dtype)], axis=1)
    qpos = past + jnp.arange(T)
    kpos = jnp.arange(L)
    sc = indexer_scores(iq, iw, ik_all)
    sc = jnp.where(kpos[None, None, :] <= qpos[None, :, None], sc, -jnp.inf)
    _, idx = lax.top_k(sc, topk)
    valid = idx <= qpos[None, :, None]
    in_past = (idx < past)[..., None, None]
    pidx = jnp.minimum(idx, past - 1)
    phys = jnp.take_along_axis(page_table, (pidx // PAGE_SIZE).reshape(DB, -1), axis=1).reshape(pidx.shape)
    off = pidx % PAGE_SIZE
    nidx = jnp.clip(idx - past, 0, T - 1)
    k_sel = jnp.where(in_past, cache_k[phys, off], _gather_rows(dk, nidx).astype(cache_k.dtype))
    v_sel = jnp.where(in_past, cache_v[phys, off], _gather_rows(dv, nidx).astype(cache_v.dtype))
    return sparse_attend(dq, k_sel, v_sel, valid)


def hmoe(h, w_rg, b_rg, w_re, b_re, w_eg, w_eu, w_ed):
    hf = h.astype(jnp.float32)
    lead = h.shape[:-1]
    p_group = jax.nn.softmax(hf @ w_rg.astype(jnp.float32) + b_rg.astype(jnp.float32), axis=-1)
    pg, grp = lax.top_k(p_group, 1)
    le = (hf @ w_re.astype(jnp.float32) + b_re.astype(jnp.float32)).reshape(lead + (N_GROUPS, EXPERTS_PER_GROUP))
    gidx = jnp.broadcast_to(grp[..., None], lead + (1, EXPERTS_PER_GROUP))
    le_g = jnp.take_along_axis(le, gidx, axis=-2)[..., 0, :]
    we, ei = lax.top_k(jax.nn.softmax(le_g, axis=-1), TOP_K_IN_GROUP)
    weights = pg * we / jnp.sum(we, axis=-1, keepdims=True)
    eid = grp * EXPERTS_PER_GROUP + ei
    gates = jnp.sum(jax.nn.one_hot(eid, N_EXPERTS, dtype=jnp.float32) * weights[..., None], axis=-2)
    out = jnp.zeros_like(h)
    for e in range(N_EXPERTS):
        a = jax.nn.silu(h @ w_eg[e]) * (h @ w_eu[e])
        out = out + gates[..., e:e + 1].astype(h.dtype) * (a @ w_ed[e])
    return out


def layer_forward(x, c, gla_state0, dsa_fn, gla_chunk_len, w_ada, b_ada, norm1_g, w_in,
                  w_gla_gate2, b_gla_gate, gla_norm_g, w_out, norm2_g, w_rg, b_rg, w_re, b_re,
                  w_eg, w_eu, w_ed):
    sh_a, sc_a, g_a, sh_m, sc_m, g_m = ada_modulation(c, w_ada, b_ada)
    h = modulate(rms_norm(x, norm1_g), sh_a, sc_a)
    gq, gk, gv, glog, gr, dq, dk, dv, iq, ik, iw = project_heads(h, w_in, w_gla_gate2, b_gla_gate)
    gla_state, go = gla_scan(gla_state0, gq, gk, gv, glog, gla_chunk_len)
    mix = jnp.concatenate([gla_output(go, gr, gla_norm_g, x.dtype), dsa_fn(dq, dk, dv, iq, ik, iw)], axis=-1)
    x = x + g_a * (mix @ w_out)
    h2 = modulate(rms_norm(x, norm2_g), sh_m, sc_m)
    x = x + g_m * hmoe(h2, w_rg, b_rg, w_re, b_re, w_eg, w_eu, w_ed)
    return x, gla_state.astype(x.dtype), dk, dv, ik


def setup_inputs(seed: int = 0) -> dict:
    key = jax.random.key(seed)
    ks = jax.random.split(key, 32)
    f32 = jnp.float32
    n_pages = PAST_LEN // PAGE_SIZE
    used = DEC_BATCH * n_pages
    n_phys = used + max(1, used // 4)
    nrm = lambda k, shape, s: jax.random.normal(k, shape, f32) * s
    page_table = jax.random.permutation(ks[0], n_phys)[:used].reshape(DEC_BATCH, n_pages).astype(jnp.int32)
    return {
        "x_prompt": nrm(ks[1], (BATCH, SEQ, D_MODEL), 1.0),
        "x_sample": nrm(ks[2], (DEC_BATCH, DEC_SEQ, D_MODEL), 1.0),
        "cache_k": nrm(ks[3], (DEPTH, n_phys, PAGE_SIZE, DSA_KV_HEADS, DSA_HEAD_DIM), 1.0),
        "cache_v": nrm(ks[4], (DEPTH, n_phys, PAGE_SIZE, DSA_KV_HEADS, DSA_HEAD_DIM), 1.0),
        "cache_idx_k": nrm(ks[5], (DEPTH, n_phys, PAGE_SIZE, IDX_DIM), 1.0),
        "state_gla": nrm(ks[6], (DEPTH, DEC_BATCH, GLA_HEADS, GLA_DK, GLA_DV), 1.0),
        "page_table": page_table,
        "c_prompt": nrm(ks[7], (BATCH, D_MODEL), 1.0),
        "c_sample": nrm(ks[8], (DEC_BATCH, D_MODEL), 1.0),
        "w_ada": nrm(ks[9], (DEPTH, D_MODEL, 6 * D_MODEL), 0.5 * D_MODEL ** -0.5),
        "b_ada": nrm(ks[10], (DEPTH, 6 * D_MODEL), 0.02),
        "norm1_g": 1.0 + nrm(ks[11], (DEPTH, D_MODEL), 0.02),
        "w_in": nrm(ks[12], (DEPTH, D_MODEL, IN_TOTAL), D_MODEL ** -0.5),
        "w_gla_gate2": nrm(ks[13], (DEPTH, GLA_GATE_RANK, GLA_KEY), GLA_GATE_RANK ** -0.5),
        "b_gla_gate": nrm(ks[14], (DEPTH, GLA_KEY), 0.02),
        "gla_norm_g": 1.0 + nrm(ks[15], (DEPTH, GLA_DV), 0.02),
        "w_out": nrm(ks[16], (DEPTH, MIX_WIDTH, D_MODEL), MIX_WIDTH ** -0.5),
        "norm2_g": 1.0 + nrm(ks[17], (DEPTH, D_MODEL), 0.02),
        "w_route_group": nrm(ks[18], (DEPTH, D_MODEL, N_GROUPS), D_MODEL ** -0.5),
        "b_route_group": nrm(ks[19], (DEPTH, N_GROUPS), 0.01),
        "w_route_expert": nrm(ks[20], (DEPTH, D_MODEL, N_EXPERTS), D_MODEL ** -0.5),
        "b_route_expert": nrm(ks[21], (DEPTH, N_EXPERTS), 0.01),
        "w_exp_gate": nrm(ks[22], (DEPTH, N_EXPERTS, D_MODEL, D_EXPERT), D_MODEL ** -0.5),
        "w_exp_up": nrm(ks[23], (DEPTH, N_EXPERTS, D_MODEL, D_EXPERT), D_MODEL ** -0.5),
        "w_exp_down": nrm(ks[24], (DEPTH, N_EXPERTS, D_EXPERT, D_MODEL), D_EXPERT ** -0.5),
        "final_norm_g": 1.0 + nrm(ks[25], (D_MODEL,), 0.02),
    }


def reference(x_prompt, x_sample, cache_k, cache_v, cache_idx_k, state_gla, page_table,
              c_prompt, c_sample, w_ada, b_ada, norm1_g, w_in, w_gla_gate2, b_gla_gate,
              gla_norm_g, w_out, norm2_g, w_route_group, b_route_group, w_route_expert,
              b_route_expert, w_exp_gate, w_exp_up, w_exp_down, final_norm_g):
    B, S = x_prompt.shape[:2]
    DB, T = x_sample.shape[:2]
    xp, xs = x_prompt, x_sample
    kp_l, vp_l, ikp_l, sp_l, ks_l, vs_l, iks_l, ss_l = [], [], [], [], [], [], [], []
    for l in range(DEPTH):
        shared = (w_ada[l], b_ada[l], norm1_g[l], w_in[l], w_gla_gate2[l], b_gla_gate[l],
                  gla_norm_g[l], w_out[l], norm2_g[l], w_route_group[l], b_route_group[l],
                  w_route_expert[l], b_route_expert[l], w_exp_gate[l], w_exp_up[l], w_exp_down[l])
        gla0 = jnp.zeros((B, GLA_HEADS, GLA_DK, GLA_DV), jnp.float32)
        xp, sp, kp, vp, ikp = layer_forward(xp, c_prompt, gla0, dsa_prompt, GLA_CHUNK, *shared)
        dsa_fn = functools.partial(dsa_sample, cache_k=cache_k[l], cache_v=cache_v[l],
                                   cache_ik=cache_idx_k[l], page_table=page_table)
        xs, ss, ks_, vs_, iks = layer_forward(xs, c_sample, state_gla[l], dsa_fn, T, *shared)
        npg = S // PAGE_SIZE
        kp_l.append(kp.reshape(B, npg, PAGE_SIZE, DSA_KV_HEADS, DSA_HEAD_DIM))
        vp_l.append(vp.reshape(B, npg, PAGE_SIZE, DSA_KV_HEADS, DSA_HEAD_DIM))
        ikp_l.append(ikp.reshape(B, npg, PAGE_SIZE, IDX_DIM))
        sp_l.append(sp)
        ks_l.append(ks_)
        vs_l.append(vs_)
        iks_l.append(iks)
        ss_l.append(ss)
    y_prompt = rms_norm(xp, final_norm_g)
    y_sample = rms_norm(xs, final_norm_g)
    return (y_prompt, y_sample, jnp.stack(kp_l), jnp.stack(vp_l), jnp.stack(ikp_l), jnp.stack(sp_l),
            jnp.stack(ks_l), jnp.stack(vs_l), jnp.stack(iks_l), jnp.stack(ss_l))
```

```python
import functools

import jax
import jax.numpy as jnp
from jax import lax
from jax.experimental import pallas as pl
from jax.experimental.pallas import tpu as pltpu

F32 = jnp.float32
BF16 = jnp.bfloat16
I32 = jnp.int32

GLA_HEADS = 4
GLA_DK = 64
GLA_DV = 128
GLA_KEY = GLA_HEADS * GLA_DK
GLA_WIDTH = GLA_HEADS * GLA_DV
GLA_GATE_RANK = 16
GLA_GATE_TAU = 16.0
DSA_HEADS = 8
DSA_HEAD_DIM = 64
DSA_KV_HEADS = 2
DSA_GROUPS = DSA_HEADS // DSA_KV_HEADS
DSA_WIDTH = DSA_HEADS * DSA_HEAD_DIM
DSA_KV_WIDTH = DSA_KV_HEADS * DSA_HEAD_DIM
IDX_HEADS = 4
IDX_DIM = 64
TOPK_MAX = 256
N_GROUPS = 4
EXPERTS_PER_GROUP = 4
N_EXPERTS = N_GROUPS * EXPERTS_PER_GROUP
D_EXPERT = 256
EPS = 1e-6

LANES = 128
Q_TILE = 128
GLA_CHUNK = 32
NEG_BIG = -1e30
INT_MIN = -2147483648

MISC_IK = 0
MISC_IW = IDX_DIM
MISC_GLR = IDX_DIM + IDX_HEADS

_VMEM_LIMIT = 56 * 1024 * 1024


def _cparams(sem):
    return pltpu.CompilerParams(dimension_semantics=sem, vmem_limit_bytes=_VMEM_LIMIT)


def _ada_kernel(c_ref, w_ref, b_ref, o_ref):
    c = c_ref[...]
    s = c * jax.nn.sigmoid(c)
    o_ref[...] = jnp.dot(s.astype(BF16), w_ref[...].astype(BF16), preferred_element_type=F32) + b_ref[...]


def _ada(c, w_ada, b_ada):
    n, d = c.shape
    nout = w_ada.shape[1]
    tn = 1024
    return pl.pallas_call(
        _ada_kernel,
        grid=(nout // tn,),
        in_specs=[pl.BlockSpec((n, d), lambda j: (0, 0)),
                  pl.BlockSpec((d, tn), lambda j: (0, j)),
                  pl.BlockSpec((1, tn), lambda j: (0, j))],
        out_specs=pl.BlockSpec((n, tn), lambda j: (0, j)),
        out_shape=jax.ShapeDtypeStruct((n, nout), F32),
        compiler_params=_cparams(("arbitrary",)),
        name="ada",
    )(c, w_ada, b_ada.reshape(1, nout))


_SEG_GQ = (0, 256)
_SEG_GK = (256, 512)
_SEG_GV = (512, 1024)
_SEG_GR = (1024, 1536)
_SEG_DQ = (1536, 2048)
_SEG_DK = (2048, 2176)
_SEG_DV = (2176, 2304)
_SEG_IQ = (2304, 2560)
_SEG_MISC = (2560, 2688)
_W_COLS = 2688


def _rms(x, g):
    ms = jnp.mean(x * x, axis=-1, keepdims=True)
    return x * lax.rsqrt(ms + EPS) * g


def _log_sigmoid(x):
    return jnp.minimum(x, 0.0) - jnp.log1p(jnp.exp(-jnp.abs(x)))


def _inproj_kernel(x_ref, g1_ref, sh_ref, sc_ref, w_ref, w2_ref, bg_ref,
                   gq_ref, gk_ref, glog_ref, gv_ref, gr_ref, dq_ref, dk_ref, dv_ref,
                   dk0_ref, dk1_ref, dv0_ref, dv1_ref, iq_ref, ik_ref, ikb_ref, iw_ref):
    h = _rms(x_ref[...], g1_ref[...]) * (1.0 + sc_ref[...]) + sh_ref[...]
    hb = h.astype(BF16)

    def seg(s):
        return jnp.dot(hb, w_ref[:, s[0]:s[1]], preferred_element_type=F32)

    gq_ref[...] = seg(_SEG_GQ) * (GLA_DK ** -0.5)
    gk_ref[...] = seg(_SEG_GK)
    gv_ref[...] = seg(_SEG_GV)
    gr_ref[...] = seg(_SEG_GR)
    dq_ref[...] = (seg(_SEG_DQ) * (DSA_HEAD_DIM ** -0.5)).astype(BF16)
    dk = seg(_SEG_DK)
    dv = seg(_SEG_DV)
    dk_ref[...] = dk
    dv_ref[...] = dv
    dk0_ref[...] = dk[:, :DSA_HEAD_DIM].astype(BF16)
    dk1_ref[...] = dk[:, DSA_HEAD_DIM:].astype(BF16)
    dv0_ref[...] = dv[:, :DSA_HEAD_DIM].astype(BF16)
    dv1_ref[...] = dv[:, DSA_HEAD_DIM:].astype(BF16)
    iq_ref[...] = seg(_SEG_IQ).astype(BF16)
    misc = seg(_SEG_MISC)
    ik = misc[:, MISC_IK:MISC_IK + IDX_DIM]
    ik_ref[...] = ik
    ikb_ref[...] = ik.astype(BF16)
    iw_ref[...] = misc
    gate = jnp.dot(misc.astype(BF16), w2_ref[...], preferred_element_type=F32) + bg_ref[...]
    glog_ref[...] = _log_sigmoid(gate) * (1.0 / GLA_GATE_TAU)


def _inproj(x, g1, shift, scale, w_perm, w2_pad, b_gate, rows_per_mod, tm):
    t, d = x.shape
    assert t % tm == 0
    if rows_per_mod >= tm:
        assert rows_per_mod % tm == 0
        per = rows_per_mod // tm
        mod_spec = pl.BlockSpec((None, 1, d), lambda i: (i // per, 0, 0))
    else:
        assert rows_per_mod == 1
        shift = shift.reshape(t, d)
        scale = scale.reshape(t, d)
        mod_spec = pl.BlockSpec((tm, d), lambda i: (i, 0))

    def row(width, dtype):
        return pl.BlockSpec((tm, width), lambda i: (i, 0)), jax.ShapeDtypeStruct((t, width), dtype)

    outs = [row(GLA_KEY, F32), row(GLA_KEY, F32), row(GLA_KEY, F32), row(GLA_WIDTH, F32), row(GLA_WIDTH, F32),
            row(DSA_WIDTH, BF16), row(DSA_KV_WIDTH, F32), row(DSA_KV_WIDTH, F32),
            row(DSA_HEAD_DIM, BF16), row(DSA_HEAD_DIM, BF16), row(DSA_HEAD_DIM, BF16), row(DSA_HEAD_DIM, BF16),
            row(IDX_HEADS * IDX_DIM, BF16), row(IDX_DIM, F32), row(IDX_DIM, BF16), row(LANES, F32)]
    return pl.pallas_call(
        _inproj_kernel,
        grid=(t // tm,),
        in_specs=[pl.BlockSpec((tm, d), lambda i: (i, 0)),
                  pl.BlockSpec((1, d), lambda i: (0, 0)),
                  mod_spec, mod_spec,
                  pl.BlockSpec((d, _W_COLS), lambda i: (0, 0)),
                  pl.BlockSpec((LANES, GLA_KEY), lambda i: (0, 0)),
                  pl.BlockSpec((1, GLA_KEY), lambda i: (0, 0))],
        out_specs=[o[0] for o in outs],
        out_shape=[o[1] for o in outs],
        compiler_params=_cparams(("arbitrary",)),
        name="inproj",
    )(x, g1.reshape(1, d), shift, scale, w_perm, w2_pad, b_gate.reshape(1, GLA_KEY))


def _gla_kernel(n_valid, chunk, q_ref, k_ref, g_ref, v_ref, gr_ref, gn_ref, s0_ref, o_ref, sf_ref, st_ref):
    j = pl.program_id(1)
    tb = q_ref.shape[0]
    n_chunks = tb // chunk

    @pl.when(j == 0)
    def _():
        st_ref[...] = s0_ref[...]

    row_k = lax.broadcasted_iota(I32, (GLA_KEY, GLA_WIDTH), 0)
    col_v = lax.broadcasted_iota(I32, (GLA_KEY, GLA_WIDTH), 1)
    head_sum = (row_k // GLA_DK == col_v // GLA_DV).astype(BF16)
    row_v = lax.broadcasted_iota(I32, (GLA_WIDTH, GLA_KEY), 0)
    col_k = lax.broadcasted_iota(I32, (GLA_WIDTH, GLA_KEY), 1)
    diag_mask = (row_v // GLA_DV == col_k // GLA_DK).astype(F32)
    tri = (lax.broadcasted_iota(I32, (chunk, chunk), 0) >= lax.broadcasted_iota(I32, (chunk, chunk), 1)).astype(BF16)
    t_idx = lax.broadcasted_iota(I32, (chunk, 1), 0)
    gn = gn_ref[...]

    def body(c, carry):
        r = pl.multiple_of(c * chunk, chunk)
        q = q_ref[pl.ds(r, chunk), :]
        k = k_ref[pl.ds(r, chunk), :]
        g = g_ref[pl.ds(r, chunk), :]
        v = v_ref[pl.ds(r, chunk), :]
        if n_valid < chunk:
            live = t_idx < n_valid
            k = jnp.where(live, k, 0.0)
            g = jnp.where(live, g, 0.0)
        g_hi = g.astype(BF16)
        g_r1 = g - g_hi.astype(F32)
        g_mid = g_r1.astype(BF16)
        g_lo = (g_r1 - g_mid.astype(F32)).astype(BF16)
        b = (jnp.dot(tri, g_hi, preferred_element_type=F32) + jnp.dot(tri, g_mid, preferred_element_type=F32)
             + jnp.dot(tri, g_lo, preferred_element_type=F32))
        st = st_ref[...]
        qd = (q * jnp.exp(b)).astype(BF16)
        o = lax.dot_general(qd, st.astype(BF16), (((1,), (1,)), ((), ())), preferred_element_type=F32)
        parts = [o[i * 8:(i + 1) * 8] for i in range(chunk // 8)]
        for s in range(min(chunk, n_valid)):
            r0 = (s // 8) * 8
            diff = b[r0:] - b[s:s + 1]
            diff = jnp.where(t_idx[r0:] >= s, diff, -jnp.inf)
            tile = (q[r0:] * k[s:s + 1] * jnp.exp(diff)).astype(BF16)
            a = jnp.dot(tile, head_sum, preferred_element_type=F32) * v[s:s + 1]
            for i in range(r0 // 8, chunk // 8):
                parts[i] = parts[i] + a[(i - r0 // 8) * 8:(i - r0 // 8 + 1) * 8]
        o = jnp.concatenate(parts, axis=0)
        b_last = b[chunk - 1:chunk]
        kd = (k * jnp.exp(b_last - b)).astype(BF16)
        upd = lax.dot_general(v.astype(BF16), kd, (((0,), (0,)), ((), ())), preferred_element_type=F32)
        st_ref[...] = st * jnp.exp(b_last) + upd * diag_mask
        gr = gr_ref[pl.ds(r, chunk), :]
        outs = []
        for hh in range(GLA_HEADS):
            oh = o[:, hh * GLA_DV:(hh + 1) * GLA_DV]
            ms = jnp.mean(oh * oh, axis=-1, keepdims=True)
            outs.append(oh * lax.rsqrt(ms + EPS))
        on = jnp.concatenate(outs, axis=1) * gn
        o_ref[pl.ds(r, chunk), :] = (on * (gr * jax.nn.sigmoid(gr))).astype(o_ref.dtype)
        return carry

    lax.fori_loop(0, n_chunks, body, 0)

    @pl.when(j == pl.num_programs(1) - 1)
    def _():
        sf_ref[...] = st_ref[...]


def _gla(gq, gk, glog, gv, gr, gla_norm_g, state_t, n_seq, n_valid, chunk, tb):
    t = gq.shape[0]
    t_seq = t // n_seq
    assert t_seq % tb == 0 and tb % chunk == 0
    nb = t_seq // tb
    gn = jnp.tile(gla_norm_g.astype(F32), GLA_HEADS).reshape(1, GLA_WIDTH)

    def tok(width):
        return pl.BlockSpec((tb, width), lambda b, j: (b * nb + j, 0))

    st_spec = pl.BlockSpec((None, GLA_WIDTH, GLA_KEY), lambda b, j: (b, 0, 0))
    return pl.pallas_call(
        functools.partial(_gla_kernel, n_valid, chunk),
        grid=(n_seq, nb),
        in_specs=[tok(GLA_KEY), tok(GLA_KEY), tok(GLA_KEY), tok(GLA_WIDTH), tok(GLA_WIDTH),
                  pl.BlockSpec((1, GLA_WIDTH), lambda b, j: (0, 0)), st_spec],
        out_specs=[tok(GLA_WIDTH), st_spec],
        out_shape=[jax.ShapeDtypeStruct((t, GLA_WIDTH), BF16),
                   jax.ShapeDtypeStruct((n_seq, GLA_WIDTH, GLA_KEY), F32)],
        scratch_shapes=[pltpu.VMEM((GLA_WIDTH, GLA_KEY), F32)],
        compiler_params=_cparams(("arbitrary", "arbitrary")),
        name="gla",
    )(gq, gk, glog, gv, gr, gn, state_t)


def _state_to_blockdiag_t(state):
    n = state.shape[0]
    st = jnp.swapaxes(state.astype(F32), 2, 3)
    eye = jnp.eye(GLA_HEADS, dtype=F32)
    full = st[:, :, :, None, :] * eye[None, :, None, :, None]
    return full.reshape(n, GLA_WIDTH, GLA_KEY)


def _blockdiag_t_to_state(st):
    n = st.shape[0]
    full = st.reshape(n, GLA_HEADS, GLA_DV, GLA_HEADS, GLA_DK)
    diag = jnp.stack([full[:, h, :, h, :] for h in range(GLA_HEADS)], axis=1)
    return jnp.swapaxes(diag, 2, 3)


def _sortable(x):
    bits = lax.bitcast_convert_type(x, I32)
    return bits ^ ((bits >> 31) & 0x7FFFFFFF)


def _dsa_core(topk, kt, n_tiles, qpos, iq, iw, dq, load_ik, load_k, load_v, key_ref, m_ref, l_ref, acc_ref):
    rows = iq.shape[0]
    n_sub = kt // LANES
    lane = lax.broadcasted_iota(I32, (1, kt), 1)

    iq_h = [iq[:, h * IDX_DIM:(h + 1) * IDX_DIM] for h in range(IDX_HEADS)]
    w_h = [iw[:, MISC_IW + h:MISC_IW + h + 1] * (IDX_HEADS ** -0.5 * IDX_DIM ** -0.5) for h in range(IDX_HEADS)]

    def score_tile(t, carry):
        ik = load_ik(t)
        acc = None
        for h in range(IDX_HEADS):
            logit = lax.dot_general(iq_h[h], ik, (((1,), (1,)), ((), ())), preferred_element_type=F32)
            term = jnp.maximum(logit, 0.0) * w_h[h]
            acc = term if acc is None else acc + term
        kpos = t * kt + lane
        sc = jnp.where(kpos <= qpos, acc, -jnp.inf)
        key_ref[:, pl.ds(pl.multiple_of(t * kt, kt), kt)] = _sortable(sc)
        return carry

    lax.fori_loop(0, n_tiles, score_tile, 0)

    def count_ge(cand):
        def tile_count(t, acc):
            keys = key_ref[:, pl.ds(pl.multiple_of(t * kt, kt), kt)]
            hit = (keys >= cand).astype(I32)
            for u in range(n_sub):
                acc = acc + hit[:, u * LANES:(u + 1) * LANES]
            return acc
        acc = lax.fori_loop(0, n_tiles, tile_count, jnp.zeros((rows, LANES), I32))
        return jnp.sum(acc, axis=1, keepdims=True)

    def bisect(i, thr):
        cand = thr + lax.shift_left(jnp.int32(1), 31 - i)
        return jnp.where(count_ge(cand) >= topk, cand, thr)

    thr = lax.fori_loop(0, 32, bisect, jnp.full((rows, 1), INT_MIN, I32))
    n_above = count_ge(thr + 1)
    n_ties_wanted = (topk - n_above).astype(F32)

    incl = (lax.broadcasted_iota(I32, (LANES, LANES), 0) <= lax.broadcasted_iota(I32, (LANES, LANES), 1)).astype(BF16)
    qs = []
    for n in range(DSA_KV_HEADS):
        qs.append(jnp.concatenate(
            [dq[:, (n * DSA_GROUPS + g) * DSA_HEAD_DIM:(n * DSA_GROUPS + g + 1) * DSA_HEAD_DIM] for g in range(DSA_GROUPS)],
            axis=0))
    gr_rows = DSA_GROUPS * rows

    m_ref[...] = jnp.full(m_ref.shape, NEG_BIG, F32)
    l_ref[...] = jnp.zeros(l_ref.shape, F32)
    acc_ref[...] = jnp.zeros(acc_ref.shape, F32)

    def attend_tile(t, ties_seen):
        keys = key_ref[:, pl.ds(pl.multiple_of(t * kt, kt), kt)]
        kpos = t * kt + lane
        sel_parts = []
        for u in range(n_sub):
            keys_u = keys[:, u * LANES:(u + 1) * LANES]
            tie_u = jnp.where(keys_u == thr, 1.0, 0.0)
            pref = jnp.dot(tie_u.astype(BF16), incl, preferred_element_type=F32) + ties_seen
            sel_parts.append(jnp.where(keys_u > thr, 1.0, jnp.where(pref <= n_ties_wanted, tie_u, 0.0)))
            ties_seen = ties_seen + jnp.sum(tie_u, axis=1, keepdims=True)
        sel_f = jnp.where(kpos <= qpos, jnp.concatenate(sel_parts, axis=1), 0.0)
        sel4 = jnp.concatenate([sel_f] * DSA_GROUPS, axis=0) > 0.5
        k_t = load_k(t)
        v_t = load_v(t)
        for n in range(DSA_KV_HEADS):
            m = m_ref[n]
            s = lax.dot_general(qs[n], k_t[n], (((1,), (1,)), ((), ())), preferred_element_type=F32)
            s = jnp.where(sel4, s, NEG_BIG)
            m_new = jnp.maximum(m, jnp.max(s, axis=1, keepdims=True))
            p = jnp.where(sel4, jnp.exp(s - m_new), 0.0)
            alpha = jnp.exp(m - m_new)
            m_ref[n] = m_new
            l_ref[n] = l_ref[n] * alpha + jnp.sum(p, axis=1, keepdims=True)
            acc_ref[n] = acc_ref[n] * alpha + jnp.dot(p.astype(BF16), v_t[n], preferred_element_type=F32)
        return ties_seen

    lax.fori_loop(0, n_tiles, attend_tile, jnp.zeros((rows, 1), F32))
    outs = []
    for n in range(DSA_KV_HEADS):
        o = acc_ref[n] / l_ref[n]
        for g in range(DSA_GROUPS):
            outs.append(o[g * rows:(g + 1) * rows])
    return jnp.concatenate(outs, axis=1)


def _softmax_scratch(rows):
    gr = DSA_GROUPS * rows
    return [pltpu.VMEM((DSA_KV_HEADS, gr, 1), F32), pltpu.VMEM((DSA_KV_HEADS, gr, 1), F32),
            pltpu.VMEM((DSA_KV_HEADS, gr, DSA_HEAD_DIM), F32)]


def _dsa_prompt_kernel(topk, kt, iq_ref, iw_ref, dq_ref, ik_ref, k0_ref, k1_ref, v0_ref, v1_ref, o_ref,
                       key_ref, m_ref, l_ref, acc_ref):
    qi = pl.program_id(1)
    rows = iq_ref.shape[0]
    qpos = qi * rows + lax.broadcasted_iota(I32, (rows, 1), 0)
    n_tiles = ((qi + 1) * rows + kt - 1) // kt

    def tile(ref):
        return lambda t: ref[pl.ds(pl.multiple_of(t * kt, kt), kt), :]

    out = _dsa_core(topk, kt, n_tiles, qpos, iq_ref[...], iw_ref[...], dq_ref[...],
                    tile(ik_ref),
                    lambda t: [tile(k0_ref)(t), tile(k1_ref)(t)],
                    lambda t: [tile(v0_ref)(t), tile(v1_ref)(t)],
                    key_ref, m_ref, l_ref, acc_ref)
    o_ref[...] = out.astype(o_ref.dtype)


def _dsa_prompt(iq, iw, dq, ikb, dk0, dk1, dv0, dv1, n_seq):
    t = iq.shape[0]
    s = t // n_seq
    topk = min(TOPK_MAX, s // 4)
    kt = min(256, s)
    nq = s // Q_TILE

    def qrow(width):
        return pl.BlockSpec((Q_TILE, width), lambda b, i: (b * nq + i, 0))

    def seq(width):
        return pl.BlockSpec((s, width), lambda b, i: (b, 0))

    return pl.pallas_call(
        functools.partial(_dsa_prompt_kernel, topk, kt),
        grid=(n_seq, nq),
        in_specs=[qrow(IDX_HEADS * IDX_DIM), qrow(LANES), qrow(DSA_WIDTH),
                  seq(IDX_DIM), seq(DSA_HEAD_DIM), seq(DSA_HEAD_DIM), seq(DSA_HEAD_DIM), seq(DSA_HEAD_DIM)],
        out_specs=qrow(DSA_WIDTH),
        out_shape=jax.ShapeDtypeStruct((t, DSA_WIDTH), BF16),
        scratch_shapes=[pltpu.VMEM((Q_TILE, s), I32)] + _softmax_scratch(Q_TILE),
        compiler_params=_cparams(("arbitrary", "arbitrary")),
        name="dsa_prompt",
    )(iq, iw, dq, ikb, dk0, dk1, dv0, dv1)


def _dsa_sample_kernel(topk, kt, past, n_pages, page, t_new, rows,
                       pt_ref, iq_ref, iw_ref, dq_ref, ikn_ref, kn_ref, vn_ref,
                       cik_ref, ck_ref, cv_ref, o_ref,
                       ik_buf, k_buf, v_buf, key_ref, m_ref, l_ref, acc_ref, sem):
    b = pl.program_id(0)
    nb = pl.num_programs(0)
    slot = b % 2

    def page_copies(seq, slot_):
        copies = []
        for p in range(n_pages):
            phys = pt_ref[seq, p]
            rows_p = pl.ds(p * page, page)
            copies.append(pltpu.make_async_copy(cik_ref.at[phys], ik_buf.at[slot_, rows_p, :], sem.at[slot_, 0]))
            copies.append(pltpu.make_async_copy(ck_ref.at[phys], k_buf.at[slot_, rows_p, :], sem.at[slot_, 1]))
            copies.append(pltpu.make_async_copy(cv_ref.at[phys], v_buf.at[slot_, rows_p, :], sem.at[slot_, 2]))
        return copies

    @pl.when(b == 0)
    def _():
        tail = pl.ds(past, ik_buf.shape[1] - past)
        for s_ in range(2):
            ik_buf[s_, tail, :] = jnp.zeros((ik_buf.shape[1] - past, IDX_DIM), F32)
            k_buf[s_, tail, :] = jnp.zeros((ik_buf.shape[1] - past, DSA_KV_WIDTH), F32)
            v_buf[s_, tail, :] = jnp.zeros((ik_buf.shape[1] - past, DSA_KV_WIDTH), F32)
        for c in page_copies(0, 0):
            c.start()

    @pl.when(b + 1 < nb)
    def _():
        for c in page_copies(b + 1, 1 - slot):
            c.start()

    for c in page_copies(b, slot):
        c.wait()

    new_rows = pl.ds(past, rows)
    ik_buf[slot, new_rows, :] = ikn_ref[...]
    k_buf[slot, new_rows, :] = kn_ref[...]
    v_buf[slot, new_rows, :] = vn_ref[...]

    qpos = past + lax.broadcasted_iota(I32, (rows, 1), 0)
    n_tiles = (past + t_new + kt - 1) // kt

    def rows_of(t):
        return pl.ds(pl.multiple_of(t * kt, kt), kt)

    def heads(buf):
        def load(t):
            x = buf[slot, rows_of(t), :].astype(BF16)
            return [x[:, n * DSA_HEAD_DIM:(n + 1) * DSA_HEAD_DIM] for n in range(DSA_KV_HEADS)]
        return load

    out = _dsa_core(topk, kt, n_tiles, qpos, iq_ref[...], iw_ref[...], dq_ref[...],
                    lambda t: ik_buf[slot, rows_of(t), :].astype(BF16), heads(k_buf), heads(v_buf),
                    key_ref, m_ref, l_ref, acc_ref)
    o_ref[...] = out.astype(o_ref.dtype)


def _dsa_sample(iq, iw, dq, ik_new, dk_new, dv_new, cache_ik, cache_k, cache_v, page_table, t_new, rows):
    n_seq, n_pages = page_table.shape
    page = cache_ik.shape[1]
    past = n_pages * page
    total = past + t_new
    topk = min(TOPK_MAX, total // 4)
    kt = min(512, past)
    buf_rows = ((total + kt - 1) // kt) * kt
    assert buf_rows >= past + rows
    ck = cache_k.reshape(cache_k.shape[0], page, DSA_KV_WIDTH)
    cv = cache_v.reshape(cache_v.shape[0], page, DSA_KV_WIDTH)

    def qrow(width):
        return pl.BlockSpec((rows, width), lambda b, pt: (b, 0))

    any_spec = pl.BlockSpec(memory_space=pl.ANY)
    grid_spec = pltpu.PrefetchScalarGridSpec(
        num_scalar_prefetch=1,
        grid=(n_seq,),
        in_specs=[qrow(IDX_HEADS * IDX_DIM), qrow(LANES), qrow(DSA_WIDTH),
                  qrow(IDX_DIM), qrow(DSA_KV_WIDTH), qrow(DSA_KV_WIDTH),
                  any_spec, any_spec, any_spec],
        out_specs=qrow(DSA_WIDTH),
        scratch_shapes=[pltpu.VMEM((2, buf_rows, IDX_DIM), F32),
                        pltpu.VMEM((2, buf_rows, DSA_KV_WIDTH), F32),
                        pltpu.VMEM((2, buf_rows, DSA_KV_WIDTH), F32),
                        pltpu.VMEM((rows, buf_rows), I32)] + _softmax_scratch(rows) + [
                        pltpu.SemaphoreType.DMA((2, 3))],
    )
    return pl.pallas_call(
        functools.partial(_dsa_sample_kernel, topk, kt, past, n_pages, page, t_new, rows),
        grid_spec=grid_spec,
        out_shape=jax.ShapeDtypeStruct((n_seq * rows, DSA_WIDTH), BF16),
        compiler_params=_cparams(("arbitrary",)),
        name="dsa_sample",
    )(page_table, iq, iw, dq, ik_new, dk_new, dv_new, cache_ik, ck, cv)


def _outproj_kernel(x_ref, gla_ref, dsa_ref, wo_ref, ga_ref, g2_ref, sh_ref, sc_ref, wr_ref, br_ref,
                    x1_ref, h2_ref, gates_ref):
    y = (jnp.dot(gla_ref[...], wo_ref[:GLA_WIDTH, :], preferred_element_type=F32)
         + jnp.dot(dsa_ref[...], wo_ref[GLA_WIDTH:, :], preferred_element_type=F32))
    x1 = x_ref[...] + ga_ref[...] * y
    x1_ref[...] = x1
    h2 = _rms(x1, g2_ref[...]) * (1.0 + sc_ref[...]) + sh_ref[...]
    h2b = h2.astype(BF16)
    h2_ref[...] = h2b
    logits = jnp.dot(h2b, wr_ref[...], preferred_element_type=F32) + br_ref[...]
    lane_i = lax.broadcasted_iota(I32, logits.shape, 1)
    lane = lane_i.astype(F32)
    far = float(1 << 20)
    is_grp = (lane_i >= N_EXPERTS) & (lane_i < N_EXPERTS + N_GROUPS)
    gl = jnp.where(is_grp, logits, -jnp.inf)
    gmax = jnp.max(gl, axis=1, keepdims=True)
    ge = jnp.exp(gl - gmax)
    pg = 1.0 / jnp.sum(ge, axis=1, keepdims=True)
    grp_lane = jnp.min(jnp.where(gl == gmax, lane, far), axis=1, keepdims=True)
    first = (grp_lane - N_EXPERTS) * EXPERTS_PER_GROUP
    in_grp = (lane >= first) & (lane < first + EXPERTS_PER_GROUP)
    el = jnp.where(in_grp, logits, -jnp.inf)
    emax = jnp.max(el, axis=1, keepdims=True)
    ee = jnp.exp(el - emax)
    pe = ee / jnp.sum(ee, axis=1, keepdims=True)
    pe = jnp.where(in_grp, pe, -1.0)
    w1 = jnp.max(pe, axis=1, keepdims=True)
    i1 = jnp.min(jnp.where(pe == w1, lane, far), axis=1, keepdims=True)
    pe2 = jnp.where(lane == i1, -1.0, pe)
    w2 = jnp.max(pe2, axis=1, keepdims=True)
    i2 = jnp.min(jnp.where(pe2 == w2, lane, far), axis=1, keepdims=True)
    norm = pg / (w1 + w2)
    gates_ref[...] = jnp.where(lane == i1, w1 * norm, 0.0) + jnp.where(lane == i2, w2 * norm, 0.0)


def _outproj(x, gla_o, dsa_o, w_out_b, g_a, g2, shift, scale, w_router, b_router, rows_per_mod, tm):
    t, d = x.shape
    assert t % tm == 0
    if rows_per_mod >= tm:
        per = rows_per_mod // tm
        mod_spec = pl.BlockSpec((None, 1, d), lambda i: (i // per, 0, 0))
    else:
        assert rows_per_mod == 1
        g_a, shift, scale = (a.reshape(t, d) for a in (g_a, shift, scale))
        mod_spec = pl.BlockSpec((tm, d), lambda i: (i, 0))

    def row(width):
        return pl.BlockSpec((tm, width), lambda i: (i, 0))

    return pl.pallas_call(
        _outproj_kernel,
        grid=(t // tm,),
        in_specs=[row(d), row(GLA_WIDTH), row(DSA_WIDTH),
                  pl.BlockSpec((GLA_WIDTH + DSA_WIDTH, d), lambda i: (0, 0)),
                  mod_spec, pl.BlockSpec((1, d), lambda i: (0, 0)), mod_spec, mod_spec,
                  pl.BlockSpec((d, LANES), lambda i: (0, 0)), pl.BlockSpec((1, LANES), lambda i: (0, 0))],
        out_specs=[row(d), row(d), row(LANES)],
        out_shape=[jax.ShapeDtypeStruct((t, d), F32), jax.ShapeDtypeStruct((t, d), BF16),
                   jax.ShapeDtypeStruct((t, LANES), F32)],
        compiler_params=_cparams(("arbitrary",)),
        name="outproj",
    )(x, gla_o, dsa_o, w_out_b, g_a, g2.reshape(1, d), shift, scale, w_router, b_router)


def _moe_kernel(x1_ref, h2_ref, gates_ref, gm_ref, gf_ref, wg_ref, wu_ref, wd_ref, y_ref):
    h = h2_ref[...]
    gates = gates_ref[...]
    acc = jnp.zeros(x1_ref.shape, F32)
    for e in range(N_EXPERTS):
        a = jnp.dot(h, wg_ref[e], preferred_element_type=F32)
        u = jnp.dot(h, wu_ref[e], preferred_element_type=F32)
        act = (a * jax.nn.sigmoid(a) * u).astype(BF16)
        acc = acc + gates[:, e:e + 1] * jnp.dot(act, wd_ref[e], preferred_element_type=F32)
    x2 = x1_ref[...] + gm_ref[...] * acc
    y_ref[...] = _rms(x2, gf_ref[...])


def _moe(x1, h2, gates, g_m, final_g, wg, wu, wd, rows_per_mod, tm):
    t, d = x1.shape
    assert t % tm == 0
    if rows_per_mod >= tm:
        per = rows_per_mod // tm
        mod_spec = pl.BlockSpec((None, 1, d), lambda i: (i // per, 0, 0))
    else:
        assert rows_per_mod == 1
        g_m = g_m.reshape(t, d)
        mod_spec = pl.BlockSpec((tm, d), lambda i: (i, 0))

    def row(width):
        return pl.BlockSpec((tm, width), lambda i: (i, 0))

    def whole(shape):
        return pl.BlockSpec(shape, lambda i: (0,) * len(shape), pipeline_mode=pl.Buffered(1))

    return pl.pallas_call(
        _moe_kernel,
        grid=(t // tm,),
        in_specs=[row(d), row(d), row(LANES), mod_spec, pl.BlockSpec((1, d), lambda i: (0, 0)),
                  whole(wg.shape), whole(wu.shape), whole(wd.shape)],
        out_specs=row(d),
        out_shape=jax.ShapeDtypeStruct((t, d), F32),
        compiler_params=_cparams(("arbitrary",)),
        name="moe",
    )(x1, h2, gates, g_m, final_g.reshape(1, d), wg, wu, wd)


def _permute_w_in(w_in):
    d = w_in.shape[0]
    glr0 = 2 * GLA_KEY + GLA_WIDTH
    gr0 = glr0 + GLA_GATE_RANK
    ik0 = gr0 + GLA_WIDTH + DSA_WIDTH + 2 * DSA_KV_WIDTH + IDX_HEADS * IDX_DIM
    end = ik0 + IDX_DIM + IDX_HEADS
    assert w_in.shape[1] == end
    pad = LANES - (IDX_DIM + IDX_HEADS + GLA_GATE_RANK)
    w = jnp.concatenate([w_in[:, :glr0], w_in[:, gr0:ik0], w_in[:, ik0:end], w_in[:, glr0:gr0],
                         jnp.zeros((d, pad), w_in.dtype)], axis=1)
    assert w.shape[1] == _W_COLS
    return w.astype(BF16)


def _layer(x, mods, rows_per_mod, gla_state_t, n_seq, gla_valid, gla_chunk, gla_tb, tm, tm_moe, dsa_fn, wts):
    sh_a, sc_a, g_a, sh_m, sc_m, g_m = mods
    (gq, gk, glog, gv, gr, dq, dk, dv, dk0, dk1, dv0, dv1, iq, ik, ikb, iw) = _inproj(
        x, wts["norm1_g"], sh_a, sc_a, wts["w_in"], wts["w2"], wts["b_gate"], rows_per_mod, tm)
    gla_o, state_t = _gla(gq, gk, glog, gv, gr, wts["gla_norm_g"], gla_state_t, n_seq, gla_valid, gla_chunk, gla_tb)
    dsa_o = dsa_fn(iq, iw, dq, ik, ikb, dk, dv, dk0, dk1, dv0, dv1)
    x1, h2, gates = _outproj(x, gla_o, dsa_o, wts["w_out"], g_a, wts["norm2_g"], sh_m, sc_m,
                             wts["w_router"], wts["b_router"], rows_per_mod, tm)
    y = _moe(x1, h2, gates, g_m, wts["final_g"], wts["wg"], wts["wu"], wts["wd"], rows_per_mod, tm_moe)
    return y, state_t, dk, dv, ik


def kernel(x_prompt, x_sample, cache_k, cache_v, cache_idx_k, state_gla, page_table, c_prompt, c_sample, w_ada, b_ada, norm1_g, w_in, w_gla_gate2, b_gla_gate, gla_norm_g, w_out, norm2_g, w_route_group, b_route_group, w_route_expert, b_route_expert, w_exp_gate, w_exp_up, w_exp_down, final_norm_g):
    depth = w_in.shape[0]
    assert depth == 1
    b, s, d = x_prompt.shape
    db, t_new, _ = x_sample.shape
    page = cache_idx_k.shape[2]
    l = 0

    w2 = jnp.zeros((LANES, GLA_KEY), F32).at[MISC_GLR:MISC_GLR + GLA_GATE_RANK].set(w_gla_gate2[l]).astype(BF16)
    w_router = jnp.zeros((d, LANES), F32)
    w_router = w_router.at[:, :N_EXPERTS].set(w_route_expert[l]).at[:, N_EXPERTS:N_EXPERTS + N_GROUPS].set(w_route_group[l])
    b_router = jnp.zeros((1, LANES), F32)
    b_router = b_router.at[0, :N_EXPERTS].set(b_route_expert[l]).at[0, N_EXPERTS:N_EXPERTS + N_GROUPS].set(b_route_group[l])
    wts = dict(norm1_g=norm1_g[l], w_in=_permute_w_in(w_in[l]), w2=w2, b_gate=b_gla_gate[l], gla_norm_g=gla_norm_g[l],
               w_out=w_out[l].astype(BF16), norm2_g=norm2_g[l], w_router=w_router.astype(BF16), b_router=b_router,
               final_g=final_norm_g, wg=w_exp_gate[l].astype(BF16), wu=w_exp_up[l].astype(BF16),
               wd=w_exp_down[l].astype(BF16))

    mod = _ada(jnp.concatenate([c_prompt, c_sample], axis=0), w_ada[l], b_ada[l])
    mod = mod.reshape(b + db, 6, 1, d)

    mods_p = [mod[:b, i] for i in range(6)]
    dsa_p = lambda iq, iw, dq, ik, ikb, dk, dv, dk0, dk1, dv0, dv1: _dsa_prompt(iq, iw, dq, ikb, dk0, dk1, dv0, dv1, b)
    tm_p = min(512, s)
    yp, st_p, kp, vp, ikp = _layer(
        x_prompt.reshape(b * s, d), mods_p, s, jnp.zeros((b, GLA_WIDTH, GLA_KEY), F32), b, GLA_CHUNK, GLA_CHUNK,
        min(256, s), tm_p, min(256, s), dsa_p, wts)

    rows = 16
    assert t_new <= rows
    xs = jnp.pad(x_sample, ((0, 0), (0, rows - t_new), (0, 0))).reshape(db * rows, d)
    mods_s = [jnp.broadcast_to(mod[b:, i], (db, rows, d)).reshape(db * rows, 1, d) for i in range(6)]
    dsa_s = lambda iq, iw, dq, ik, ikb, dk, dv, dk0, dk1, dv0, dv1: _dsa_sample(
        iq, iw, dq, ik, dk, dv, cache_idx_k[l], cache_k[l], cache_v[l], page_table, t_new, rows)
    tm_s = min(512, db * rows)
    ys, st_s, ks, vs, iks = _layer(
        xs, mods_s, 1, _state_to_blockdiag_t(state_gla[l]), db, t_new, rows, rows, tm_s, min(256, db * rows),
        dsa_s, wts)

    npg = s // page
    live = lambda a: a.reshape((db, rows) + a.shape[1:])[:, :t_new]
    return (yp.reshape(b, s, d),
            live(ys),
            kp.reshape(1, b, npg, page, DSA_KV_HEADS, DSA_HEAD_DIM),
            vp.reshape(1, b, npg, page, DSA_KV_HEADS, DSA_HEAD_DIM),
            ikp.reshape(1, b, npg, page, IDX_DIM),
            _blockdiag_t_to_state(st_p)[None].astype(x_prompt.dtype),
            live(ks).reshape(1, db, t_new, DSA_KV_HEADS, DSA_HEAD_DIM),
            live(vs).reshape(1, db, t_new, DSA_KV_HEADS, DSA_HEAD_DIM),
            live(iks)[None],
            _blockdiag_t_to_state(st_s)[None].astype(x_sample.dtype))
```
